```python
import math
import jax, jax.numpy as jnp
from jax import lax
import numpy as np

D_MODEL = 2048
BATCH = 8
SEQ = 2048
DEPTH = 1

CHUNK = 64
Q_BLOCK = 128
HEAD_DIM = 128
N_HEADS_TOTAL = D_MODEL // HEAD_DIM
N_MEM_HEADS = 4
N_FOX_HEADS = (N_HEADS_TOTAL - N_MEM_HEADS) // 2
N_GDN_HEADS = N_HEADS_TOTAL - N_MEM_HEADS - N_FOX_HEADS
FOX_W = N_FOX_HEADS * HEAD_DIM
GDN_W = N_GDN_HEADS * HEAD_DIM
MEM_W = N_MEM_HEADS * HEAD_DIM
MIX_W = FOX_W + GDN_W + MEM_W
MEM_LEN = 256
CONV_WIDTH = 4
FF_DIM = -(-8 * D_MODEL // (3 * 256)) * 256
NORM_EPS = 1e-6

_SIZES = (FOX_W, FOX_W, FOX_W, N_FOX_HEADS,
          3 * GDN_W, GDN_W, N_GDN_HEADS, N_GDN_HEADS,
          MEM_W)
IN_COLS = sum(_SIZES)
SPLITS = tuple(int(s) for s in np.cumsum(_SIZES)[:-1])

kernel_name = "hymba_fox_gdn_memory_layer"


def rms_norm(x, gain):
    xf = x.astype(jnp.float32)
    y = xf * lax.rsqrt(jnp.mean(xf * xf, axis=-1, keepdims=True) + NORM_EPS)
    return (y * gain.astype(jnp.float32)).astype(x.dtype)


def l2_norm(x):
    return x * lax.rsqrt(jnp.sum(x * x, axis=-1, keepdims=True) + NORM_EPS)


def causal_depthwise_conv(x, w):
    k_w, c = w.shape
    return lax.conv_general_dilated(
        x, w[:, None, :].astype(x.dtype), window_strides=(1,), padding=[(k_w - 1, 0)],
        dimension_numbers=("NWC", "WIO", "NWC"), feature_group_count=c)


def forgetting_attention(q, k, v, log_f):
    b, t, h, d = q.shape
    c = jnp.cumsum(log_f, axis=1).transpose(0, 2, 1)
    scale = d ** -0.5
    outs = []
    for i in range(t // Q_BLOCK):
        q0, q1 = i * Q_BLOCK, (i + 1) * Q_BLOCK
        s = jnp.einsum("bqhd,bkhd->bhqk", q[:, q0:q1], k[:, :q1]).astype(jnp.float32) * scale
        s = s + c[:, :, q0:q1, None] - c[:, :, None, :q1]
        mask = jnp.arange(q1)[None, :] <= (q0 + jnp.arange(Q_BLOCK))[:, None]
        p = jax.nn.softmax(jnp.where(mask, s, -jnp.inf), axis=-1)
        outs.append(jnp.einsum("bhqk,bkhd->bqhd", p.astype(v.dtype), v[:, :q1]))
    return jnp.concatenate(outs, axis=1)


def chunked_gated_delta_rule(q, k, v, g, beta):
    b, t, h, dk = q.shape
    dv = v.shape[-1]
    n = t // CHUNK

    def to_chunks(a):
        return a.reshape(b, n, CHUNK, h, -1).transpose(1, 0, 3, 2, 4)

    q, k, v = to_chunks(q), to_chunks(k), to_chunks(v)
    g = g.reshape(b, n, CHUNK, h).transpose(1, 0, 3, 2)
    beta = beta.reshape(b, n, CHUNK, h).transpose(1, 0, 3, 2)
    gc = jnp.cumsum(g, axis=-1)
    tril = jnp.tril(jnp.ones((CHUNK, CHUNK), bool))
    strict = jnp.tril(jnp.ones((CHUNK, CHUNK), bool), -1)
    decay = jnp.exp(jnp.where(tril, gc[..., :, None] - gc[..., None, :], -jnp.inf))
    kb = k * beta[..., None]
    vb = v * beta[..., None]
    lower = jnp.where(strict, jnp.einsum("nbhcd,nbhed->nbhce", kb, k) * decay, 0.0)
    a_mat = lower + jnp.eye(CHUNK, dtype=lower.dtype)
    rhs = jnp.concatenate([vb, kb * jnp.exp(gc)[..., None]], axis=-1)
    sol = lax.linalg.triangular_solve(a_mat, rhs, left_side=True, lower=True, unit_diagonal=True)
    u, w = sol[..., :dv], sol[..., dv:]
    attn_intra = jnp.where(tril, jnp.einsum("nbhcd,nbhed->nbhce", q, k) * decay, 0.0)
    qg = q * jnp.exp(gc)[..., None]
    g_last = gc[..., -1]
    kdec = k * jnp.exp(g_last[..., None] - gc)[..., None]

    def step(state, inp):
        u_i, w_i, qg_i, at_i, kd_i, gl_i = inp
        v_new = u_i - jnp.einsum("bhcd,bhde->bhce", w_i, state)
        o = jnp.einsum("bhcd,bhde->bhce", qg_i, state) + jnp.einsum("bhcs,bhse->bhce", at_i, v_new)
        state = state * jnp.exp(gl_i)[..., None, None] + jnp.einsum("bhcd,bhce->bhde", kd_i, v_new)
        return state, o

    s0 = jnp.zeros((b, h, dk, dv), jnp.float32)
    _, o = lax.scan(step, s0, (u, w, qg, attn_intra, kdec, g_last))
    return o.transpose(1, 0, 3, 2, 4).reshape(b, t, h, dv)


def gated_deltanet(qkv, z, a, bt, conv_w, a_log, dt_bias, out_gain):
    b, t, _ = qkv.shape
    qkv = jax.nn.silu(causal_depthwise_conv(qkv, conv_w))
    q, k, v = jnp.split(qkv, 3, axis=-1)
    q = l2_norm(q.reshape(b, t, N_GDN_HEADS, HEAD_DIM).astype(jnp.float32)) * (HEAD_DIM ** -0.5)
    k = l2_norm(k.reshape(b, t, N_GDN_HEADS, HEAD_DIM).astype(jnp.float32))
    v = v.reshape(b, t, N_GDN_HEADS, HEAD_DIM).astype(jnp.float32)
    beta = jax.nn.sigmoid(bt.astype(jnp.float32))
    g = -jnp.exp(a_log.astype(jnp.float32)) * jax.nn.softplus(a.astype(jnp.float32) + dt_bias.astype(jnp.float32))
    o = chunked_gated_delta_rule(q, k, v, g, beta)
    o = rms_norm(o, out_gain) * jax.nn.silu(z.reshape(b, t, N_GDN_HEADS, HEAD_DIM).astype(jnp.float32))
    return o.reshape(b, t, GDN_W).astype(qkv.dtype)


def memory_attention(q, k, v):
    s = jnp.einsum("bthd,bmhd->bhtm", q, k).astype(jnp.float32) * (q.shape[-1] ** -0.5)
    p = jax.nn.softmax(s, axis=-1)
    return jnp.einsum("bhtm,bmhd->bthd", p.astype(v.dtype), v)


def setup_inputs(seed: int = 0) -> dict:
    key = jax.random.key(seed)
    ks = jax.random.split(key, 24)
    nrm = lambda k, shape, s: jax.random.normal(k, shape, jnp.float32) * s
    gain = lambda k, shape: 1.0 + 0.02 * jax.random.normal(k, shape, jnp.float32)
    dt = jnp.exp(jax.random.uniform(ks[9], (DEPTH, N_GDN_HEADS), jnp.float32,
                                    minval=math.log(1e-3), maxval=math.log(1e-1)))
    return {
        "x": nrm(ks[0], (BATCH, SEQ, D_MODEL), 1.0),
        "mem": nrm(ks[1], (BATCH, MEM_LEN, D_MODEL), 1.0),
        "norm_mix": gain(ks[2], (DEPTH, D_MODEL)),
        "w_in": nrm(ks[3], (DEPTH, D_MODEL, IN_COLS), D_MODEL ** -0.5),
        "fox_f_bias": 2.0 + 0.1 * jax.random.normal(ks[4], (DEPTH, N_FOX_HEADS), jnp.float32),
        "fox_q_norm": gain(ks[5], (DEPTH, HEAD_DIM)),
        "fox_k_norm": gain(ks[6], (DEPTH, HEAD_DIM)),
        "gdn_conv": nrm(ks[7], (DEPTH, CONV_WIDTH, 3 * GDN_W), CONV_WIDTH ** -0.5),
        "gdn_a_log": jnp.log(jax.random.uniform(ks[8], (DEPTH, N_GDN_HEADS), jnp.float32, minval=1.0, maxval=16.0)),
        "gdn_dt_bias": dt + jnp.log(-jnp.expm1(-dt)),
        "gdn_out_norm": gain(ks[10], (DEPTH, HEAD_DIM)),
        "mem_norm": gain(ks[11], (DEPTH, D_MODEL)),
        "w_mem_kv": nrm(ks[12], (DEPTH, D_MODEL, 2 * MEM_W), D_MODEL ** -0.5),
        "mem_q_norm": gain(ks[13], (DEPTH, HEAD_DIM)),
        "mem_k_norm": gain(ks[14], (DEPTH, HEAD_DIM)),
        "w_out": nrm(ks[15], (DEPTH, MIX_W, D_MODEL), MIX_W ** -0.5),
        "norm_ffn": gain(ks[16], (DEPTH, D_MODEL)),
        "w_gate_up": nrm(ks[17], (DEPTH, D_MODEL, 2 * FF_DIM), D_MODEL ** -0.5),
        "w_down": nrm(ks[18], (DEPTH, FF_DIM, D_MODEL), FF_DIM ** -0.5),
    }


def reference(x, mem, norm_mix, w_in, fox_f_bias, fox_q_norm, fox_k_norm, gdn_conv, gdn_a_log,
              gdn_dt_bias, gdn_out_norm, mem_norm, w_mem_kv, mem_q_norm, mem_k_norm, w_out,
              norm_ffn, w_gate_up, w_down):
    b, t, _ = x.shape
    m = mem.shape[1]
    h = x
    for l in range(DEPTH):
        u = rms_norm(h, norm_mix[l])
        p = u @ w_in[l]
        fq, fk, fv, ff, gqkv, gz, ga, gb, mq = jnp.split(p, SPLITS, axis=-1)

        fq = rms_norm(fq.reshape(b, t, N_FOX_HEADS, HEAD_DIM), fox_q_norm[l])
        fk = rms_norm(fk.reshape(b, t, N_FOX_HEADS, HEAD_DIM), fox_k_norm[l])
        fv = fv.reshape(b, t, N_FOX_HEADS, HEAD_DIM)
        log_f = jax.nn.log_sigmoid(ff.astype(jnp.float32) + fox_f_bias[l].astype(jnp.float32))
        o_fox = forgetting_attention(fq, fk, fv, log_f).reshape(b, t, FOX_W)

        o_gdn = gated_deltanet(gqkv, gz, ga, gb, gdn_conv[l], gdn_a_log[l], gdn_dt_bias[l], gdn_out_norm[l])

        mkv = rms_norm(mem, mem_norm[l]) @ w_mem_kv[l]
        mk, mv = jnp.split(mkv, 2, axis=-1)
        mq = rms_norm(mq.reshape(b, t, N_MEM_HEADS, HEAD_DIM), mem_q_norm[l])
        mk = rms_norm(mk.reshape(b, m, N_MEM_HEADS, HEAD_DIM), mem_k_norm[l])
        mv = mv.reshape(b, m, N_MEM_HEADS, HEAD_DIM)
        o_mem = memory_attention(mq, mk, mv).reshape(b, t, MEM_W)

        mix = jnp.concatenate([o_fox, o_gdn.astype(o_fox.dtype), o_mem], axis=-1)
        h = h + mix @ w_out[l]

        gate, up = jnp.split(rms_norm(h, norm_ffn[l]) @ w_gate_up[l], 2, axis=-1)
        h = h + (jax.nn.silu(gate) * up) @ w_down[l]
    return h
```

```python
import functools

import jax
import jax.numpy as jnp
from jax import lax
from jax.experimental import pallas as pl
from jax.experimental.pallas import tpu as pltpu

F32 = jnp.float32
BF16 = jnp.bfloat16

D_MODEL = 2048
HEAD_DIM = 128
N_FOX = 6
N_GDN = 6
N_MEM = 4
FOX_W = N_FOX * HEAD_DIM
GDN_W = N_GDN * HEAD_DIM
MEM_W = N_MEM * HEAD_DIM
CHUNK = 64
CONV_WIDTH = 4
NORM_EPS = 1e-6
FF_DIM = 5632

LANES = 128
SUBLANES = 8
MXU_WIDTH = 256
VMEM_LIMIT_CAP = 60000 * 1024

MAIN_W = 3 * FOX_W + 3 * GDN_W + GDN_W
SMALL_W = LANES
PROJ_W = MAIN_W + MEM_W + SMALL_W
G_LANE = 8
B_LANE = 16

NT_DIMS = (((1,), (1,)), ((), ()))
TN_DIMS = (((0,), (0,)), ((), ()))


def _vmem_limit(nbytes):
    return int(min(VMEM_LIMIT_CAP, nbytes))


def _rms(x, gain):
    ms = jnp.mean(x * x, axis=-1, keepdims=True)
    return x * lax.rsqrt(ms + NORM_EPS) * gain


def _silu(x):
    return x * jax.nn.sigmoid(x)


def _softplus(x):
    return jnp.maximum(x, 0.0) + jnp.log1p(jnp.exp(-jnp.abs(x)))


def _norm_rows_to(x_ref, g_ref, xn_ref, row_chunk):
    tm = x_ref.shape[0]

    def body(r, c):
        rows = pl.ds(pl.multiple_of(r * row_chunk, row_chunk), row_chunk)
        xn_ref[rows, :] = _rms(x_ref[rows, :], g_ref[...]).astype(BF16)
        return c

    lax.fori_loop(0, tm // row_chunk, body, 0)


def _norm_proj_kernel(x_ref, g_ref, w_ref, o_main, o_mq, o_small, xn_ref):
    _norm_rows_to(x_ref, g_ref, xn_ref, 64)
    for c in range(MAIN_W // MXU_WIDTH):
        cols = slice(c * MXU_WIDTH, (c + 1) * MXU_WIDTH)
        o_main[:, cols] = jnp.dot(xn_ref[...], w_ref[:, cols],
                                  preferred_element_type=F32).astype(o_main.dtype)
    for c in range(MEM_W // MXU_WIDTH):
        cols = slice(MAIN_W + c * MXU_WIDTH, MAIN_W + (c + 1) * MXU_WIDTH)
        o_mq[:, c * MXU_WIDTH:(c + 1) * MXU_WIDTH] = jnp.dot(
            xn_ref[...], w_ref[:, cols], preferred_element_type=F32).astype(o_mq.dtype)
    o_small[...] = jnp.dot(xn_ref[...], w_ref[:, MAIN_W + MEM_W:],
                           preferred_element_type=F32)


def _in_proj(x2d, gain, w_all, tm=512):
    m, k = x2d.shape
    vmem = (2 * tm * k * 4 + k * PROJ_W * 2 + 2 * tm * (MAIN_W + MEM_W) * 2
            + 2 * tm * SMALL_W * 4 + tm * k * 2 + 4 * tm * MXU_WIDTH * 4 + (4 << 20))
    return pl.pallas_call(
        _norm_proj_kernel,
        grid=(m // tm,),
        in_specs=[
            pl.BlockSpec((tm, k), lambda i: (i, 0)),
            pl.BlockSpec((1, k), lambda i: (0, 0)),
            pl.BlockSpec((k, PROJ_W), lambda i: (0, 0), pipeline_mode=pl.Buffered(1)),
        ],
        out_specs=[
            pl.BlockSpec((tm, MAIN_W), lambda i: (i, 0)),
            pl.BlockSpec((tm, MEM_W), lambda i: (i, 0)),
            pl.BlockSpec((tm, SMALL_W), lambda i: (i, 0)),
        ],
        out_shape=[
            jax.ShapeDtypeStruct((m, MAIN_W), BF16),
            jax.ShapeDtypeStruct((m, MEM_W), BF16),
            jax.ShapeDtypeStruct((m, SMALL_W), F32),
        ],
        scratch_shapes=[pltpu.VMEM((tm, k), BF16)],
        compiler_params=pltpu.CompilerParams(
            dimension_semantics=("arbitrary",), vmem_limit_bytes=_vmem_limit(vmem)),
        name="in_proj",
    )(x2d, gain, w_all)


def _norm_mm_kernel(x_ref, g_ref, w_ref, o_ref, xn_ref):
    _norm_rows_to(x_ref, g_ref, xn_ref, 64)
    n = w_ref.shape[1]
    for c in range(n // MXU_WIDTH):
        cols = slice(c * MXU_WIDTH, (c + 1) * MXU_WIDTH)
        o_ref[:, cols] = jnp.dot(xn_ref[...], w_ref[:, cols],
                                 preferred_element_type=F32).astype(o_ref.dtype)


def _norm_mm(x2d, gain, w, tm=512):
    m, k = x2d.shape
    n = w.shape[1]
    vmem = (2 * tm * k * 4 + 2 * k * n * 2 + 2 * tm * n * 2 + tm * k * 2
            + 4 * tm * MXU_WIDTH * 4 + (4 << 20))
    return pl.pallas_call(
        _norm_mm_kernel,
        grid=(m // tm,),
        in_specs=[
            pl.BlockSpec((tm, k), lambda i: (i, 0)),
            pl.BlockSpec((1, k), lambda i: (0, 0)),
            pl.BlockSpec((k, n), lambda i: (0, 0)),
        ],
        out_specs=pl.BlockSpec((tm, n), lambda i: (i, 0)),
        out_shape=jax.ShapeDtypeStruct((m, n), BF16),
        scratch_shapes=[pltpu.VMEM((tm, k), BF16)],
        compiler_params=pltpu.CompilerParams(
            dimension_semantics=("arbitrary",), vmem_limit_bytes=_vmem_limit(vmem)),
        name="mem_kv_proj",
    )(x2d, gain, w)


def _gates_kernel(ps_ref, add_ref, alog_ref, col_ref, row_ref):
    t = ps_ref.shape[1]
    lane = lax.broadcasted_iota(jnp.int32, (SUBLANES, LANES), 1)
    sub = lax.broadcasted_iota(jnp.int32, (SUBLANES, LANES), 0)
    is_f = lane < G_LANE
    is_g = (lane >= G_LANE) & (lane < B_LANE)
    addv = add_ref[...]
    neg_a = -jnp.exp(alog_ref[...])
    groups_per_chunk = CHUNK // SUBLANES

    def body(r, carry):
        rows = pl.ds(pl.multiple_of(r * SUBLANES, SUBLANES), SUBLANES)
        x = ps_ref[0, rows, :]
        y = x + addv
        tail = jnp.log1p(jnp.exp(-jnp.abs(y)))
        log_f = -(jnp.maximum(-y, 0.0) + tail)
        g = neg_a * (jnp.maximum(y, 0.0) + tail)
        beta = jax.nn.sigmoid(x)
        v = jnp.where(is_f, log_f, jnp.where(is_g, g, 0.0))
        s = v + jnp.where(sub >= 1, pltpu.roll(v, 1, 0), 0.0)
        s = s + jnp.where(sub >= 2, pltpu.roll(s, 2, 0), 0.0)
        s = s + jnp.where(sub >= 4, pltpu.roll(s, 4, 0), 0.0)
        reset = is_g[0:1, :] & (r % groups_per_chunk == 0)
        carry = jnp.where(reset, 0.0, carry)
        out = s + carry
        col_ref[0, rows, :] = jnp.where(lane >= B_LANE, beta, out)
        return out[SUBLANES - 1:SUBLANES, :]

    lax.fori_loop(0, t // SUBLANES, body, jnp.zeros((1, LANES), F32))

    for c in range(t // LANES):
        blk = col_ref[0, c * LANES:(c + 1) * LANES, :]
        row_ref[0, :, c * LANES:(c + 1) * LANES] = blk.T[0:2 * SUBLANES, :]


def _gates(p_small, addv, alogv):
    b, t, _ = p_small.shape
    return pl.pallas_call(
        _gates_kernel,
        grid=(b,),
        in_specs=[
            pl.BlockSpec((1, t, LANES), lambda i: (i, 0, 0)),
            pl.BlockSpec((1, LANES), lambda i: (0, 0)),
            pl.BlockSpec((1, LANES), lambda i: (0, 0)),
        ],
        out_specs=[
            pl.BlockSpec((1, t, LANES), lambda i: (i, 0, 0)),
            pl.BlockSpec((1, 2 * SUBLANES, t), lambda i: (i, 0, 0)),
        ],
        out_shape=[
            jax.ShapeDtypeStruct((b, t, LANES), F32),
            jax.ShapeDtypeStruct((b, 2 * SUBLANES, t), F32),
        ],
        compiler_params=pltpu.CompilerParams(dimension_semantics=("arbitrary",)),
        name="gates",
    )(p_small, addv, alogv)


def _fox_kernel(q_ref, k_ref, v_ref, ccol_ref, crow_ref, gq_ref, gk_ref, o_ref,
                kn_ref, qn_ref, cq_ref, m_ref, l_ref, acc_ref, *, tq):
    qi = pl.program_id(1)
    t = k_ref.shape[1]
    scale = HEAD_DIM ** -0.5

    @pl.when(qi == 0)
    def _():
        def body(r, c):
            rows = pl.ds(pl.multiple_of(r * tq, tq), tq)
            for h in range(N_FOX):
                hs = slice(h * HEAD_DIM, (h + 1) * HEAD_DIM)
                kn_ref[rows, hs] = _rms(k_ref[0, rows, hs].astype(F32), gk_ref[...]).astype(BF16)
            return c
        lax.fori_loop(0, t // tq, body, 0)

    q0 = pl.multiple_of(qi * tq, tq)
    ccol = ccol_ref[0]
    lane = lax.broadcasted_iota(jnp.int32, (tq, LANES), 1)
    for h in range(N_FOX):
        hs = slice(h * HEAD_DIM, (h + 1) * HEAD_DIM)
        qn_ref[:, hs] = (_rms(q_ref[0, :, hs].astype(F32), gq_ref[...]) * scale).astype(BF16)
        cq = jnp.sum(jnp.where(lane == h, ccol, 0.0), axis=-1, keepdims=True)
        cq_ref[h] = jnp.broadcast_to(cq, (tq, LANES))
        m_ref[h] = jnp.full((tq, LANES), -jnp.inf, F32)
        l_ref[h] = jnp.zeros((tq, LANES), F32)
        acc_ref[h] = jnp.zeros((tq, HEAD_DIM), F32)

    def tile(k0, masked):
        krows = pl.ds(k0, tq)
        for h in range(N_FOX):
            hs = slice(h * HEAD_DIM, (h + 1) * HEAD_DIM)
            s = lax.dot_general(qn_ref[:, hs], kn_ref[krows, hs], NT_DIMS,
                                preferred_element_type=F32)
            ck = crow_ref[0, h:h + 1, krows]
            cqh = cq_ref[h]
            bias = jnp.concatenate([cqh] * (tq // LANES), axis=1) - ck
            s = s + bias
            if masked:
                ri = lax.broadcasted_iota(jnp.int32, (tq, tq), 0)
                ci = lax.broadcasted_iota(jnp.int32, (tq, tq), 1)
                s = jnp.where(ci <= ri, s, -jnp.inf)
            m_old = m_ref[h]
            m_new = jnp.maximum(m_old, jnp.max(s, axis=-1, keepdims=True))
            alpha = jnp.exp(m_old - m_new)
            p = jnp.exp(s - jnp.concatenate([m_new] * (tq // LANES), axis=1))
            l_ref[h] = alpha * l_ref[h] + jnp.sum(p, axis=-1, keepdims=True)
            acc_ref[h] = alpha * acc_ref[h] + jnp.dot(
                p.astype(BF16), v_ref[0, krows, hs], preferred_element_type=F32)
            m_ref[h] = m_new

    def kbody(j, c):
        tile(pl.multiple_of(j * tq, tq), False)
        return c

    lax.fori_loop(0, qi, kbody, 0)
    tile(q0, True)

    for h in range(N_FOX):
        hs = slice(h * HEAD_DIM, (h + 1) * HEAD_DIM)
        o_ref[0, :, hs] = (acc_ref[h] / l_ref[h]).astype(o_ref.dtype)


def _fox_attn(p_main3, gates_col, gates_row, gq, gk, tq=256):
    b, t, _ = p_main3.shape
    vmem = (2 * tq * FOX_W * 2 + 4 * t * FOX_W * 2 + 2 * tq * LANES * 4 + 2 * 16 * t * 4
            + 2 * tq * FOX_W * 2 + t * FOX_W * 2 + tq * FOX_W * 2
            + 4 * N_FOX * tq * LANES * 4 + 16 * tq * tq * 4 + (4 << 20))
    return pl.pallas_call(
        functools.partial(_fox_kernel, tq=tq),
        grid=(b, t // tq),
        in_specs=[
            pl.BlockSpec((1, tq, FOX_W), lambda i, j: (i, j, 0)),
            pl.BlockSpec((1, t, FOX_W), lambda i, j: (i, 0, 1)),
            pl.BlockSpec((1, t, FOX_W), lambda i, j: (i, 0, 2)),
            pl.BlockSpec((1, tq, LANES), lambda i, j: (i, j, 0)),
            pl.BlockSpec((1, 2 * SUBLANES, t), lambda i, j: (i, 0, 0)),
            pl.BlockSpec((1, HEAD_DIM), lambda i, j: (0, 0)),
            pl.BlockSpec((1, HEAD_DIM), lambda i, j: (0, 0)),
        ],
        out_specs=pl.BlockSpec((1, tq, FOX_W), lambda i, j: (i, j, 0)),
        out_shape=jax.ShapeDtypeStruct((b, t, FOX_W), BF16),
        scratch_shapes=[
            pltpu.VMEM((t, FOX_W), BF16),
            pltpu.VMEM((tq, FOX_W), BF16),
            pltpu.VMEM((N_FOX, tq, LANES), F32),
            pltpu.VMEM((N_FOX, tq, LANES), F32),
            pltpu.VMEM((N_FOX, tq, LANES), F32),
            pltpu.VMEM((N_FOX, tq, HEAD_DIM), F32),
        ],
        compiler_params=pltpu.CompilerParams(
            dimension_semantics=("arbitrary", "arbitrary"),
            vmem_limit_bytes=_vmem_limit(vmem)),
        name="fox_attn",
    )(p_main3, p_main3, p_main3, gates_col, gates_row, gq, gk)


GDN_BLOCK = 2 * CHUNK
GDN_PAD = SUBLANES


def _gdn_kernel(q_ref, k_ref, v_ref, z_ref, gcol_ref, grow_ref, wq_ref, wk_ref, wv_ref,
                gain_ref, o_ref, xpad_ref, s_ref):
    h = pl.program_id(1)
    t = q_ref.shape[1]
    rb = GDN_BLOCK

    xpad_ref[0:GDN_PAD, :] = jnp.zeros((GDN_PAD, 3 * HEAD_DIM), F32)
    for part, ref in enumerate((q_ref, k_ref, v_ref)):
        xpad_ref[GDN_PAD:GDN_PAD + t, part * HEAD_DIM:(part + 1) * HEAD_DIM] = ref[0].astype(F32)
    s_ref[...] = jnp.zeros((HEAD_DIM, HEAD_DIM), F32)

    lane = lax.broadcasted_iota(jnp.int32, (rb, LANES), 1)
    sel_g = lane == (G_LANE + h)
    sel_b = lane == (B_LANE + h)
    sel_row = lax.broadcasted_iota(jnp.int32, (SUBLANES, rb), 0) == h
    ri =lax.broadcasted_iota(jnp.int32, (rb, rb), 0)
    ci = lax.broadcasted_iota(jnp.int32, (rb, rb), 1)
    same = (ri // CHUNK) == (ci // CHUNK)
    tril = same & (ci <= ri)
    strict = same & (ci < ri)

    def conv(r0, part, w_ref):
        acc = None
        xwin = xpad_ref[pl.ds(r0, rb + GDN_PAD), part * HEAD_DIM:(part + 1) * HEAD_DIM]
        for j in range(CONV_WIDTH):
            lo = GDN_PAD - (CONV_WIDTH - 1) + j
            term = xwin[lo:lo + rb, :] * w_ref[j:j + 1, :]
            acc = term if acc is None else acc + term
        return _silu(acc)

    def block(r, c):
        r0 = pl.multiple_of(r * rb, rb)
        rows = pl.ds(r0, rb)
        q = conv(r0, 0, wq_ref)
        k = conv(r0, 1, wk_ref)
        v = conv(r0, 2, wv_ref)
        q = q * lax.rsqrt(jnp.sum(q * q, axis=-1, keepdims=True) + NORM_EPS) * (HEAD_DIM ** -0.5)
        k = k * lax.rsqrt(jnp.sum(k * k, axis=-1, keepdims=True) + NORM_EPS)
        gt = gcol_ref[0, rows, :]
        gc = jnp.sum(jnp.where(sel_g, gt, 0.0), axis=-1, keepdims=True)
        beta = jnp.sum(jnp.where(sel_b, gt, 0.0), axis=-1, keepdims=True)
        grows = grow_ref[0, SUBLANES:2 * SUBLANES, rows]
        grow = jnp.sum(jnp.where(sel_row, grows, 0.0), axis=0, keepdims=True)
        decay = jnp.exp(jnp.where(tril, gc - grow, -jnp.inf))
        kb = k * beta
        vb = v * beta
        k16 = k.astype(BF16)
        kk = lax.dot_general(kb.astype(BF16), k16, NT_DIMS, preferred_element_type=F32)
        lower = jnp.where(strict, kk * decay, 0.0)
        eg = jnp.exp(gc)
        x = jnp.concatenate([vb, kb * eg], axis=1)
        mpow = -lower
        for it in range(6):
            m16 = mpow.astype(BF16)
            x = x + jnp.dot(m16, x.astype(BF16), preferred_element_type=F32)
            if it < 5:
                mpow = jnp.dot(m16, m16, preferred_element_type=F32)
        u = x[:, :HEAD_DIM]
        w = x[:, HEAD_DIM:]
        qk = lax.dot_general(q.astype(BF16), k16, NT_DIMS, preferred_element_type=F32)
        attn = jnp.where(tril, qk * decay, 0.0)
        qg = q * eg

        s = s_ref[...]
        outs = []
        for cc in range(rb // CHUNK):
            sl = slice(cc * CHUNK, (cc + 1) * CHUNK)
            g_last = gc[(cc + 1) * CHUNK - 1:(cc + 1) * CHUNK, :]
            kdec = k[sl] * jnp.exp(g_last - gc[sl])
            s16 = s.astype(BF16)
            v_new = u[sl] - jnp.dot(w[sl].astype(BF16), s16, preferred_element_type=F32)
            v16 = v_new.astype(BF16)
            o_c = (jnp.dot(qg[sl].astype(BF16), s16, preferred_element_type=F32)
                   + jnp.dot(attn[sl, sl].astype(BF16), v16, preferred_element_type=F32))
            s = s * jnp.exp(g_last) + jnp.dot(kdec.T.astype(BF16), v16,
                                              preferred_element_type=F32)
            outs.append(o_c)
        s_ref[...] = s
        o = _rms(jnp.concatenate(outs, axis=0), gain_ref[...])
        z = z_ref[0, rows, :].astype(F32)
        o_ref[0, rows, :] = (o * _silu(z)).astype(o_ref.dtype)
        return c

    lax.fori_loop(0, t // rb, block, 0)


def _gdn(p_main3, gates_col, gates_row, conv_w, out_gain):
    b, t, _ = p_main3.shape
    hb = lambda base: (lambda i, h: (i, 0, base + h))
    cw = lambda base: (lambda i, h: (0, base + h))
    q_blk = 3 * N_FOX
    vmem = (2 * 4 * t * HEAD_DIM * 2 + 2 * t * LANES * 4 + 2 * 16 * t * 4
            + 2 * t * HEAD_DIM * 2 + (t + GDN_PAD) * 3 * HEAD_DIM * 4 + (8 << 20))
    return pl.pallas_call(
        _gdn_kernel,
        grid=(b, N_GDN),
        in_specs=[
            pl.BlockSpec((1, t, HEAD_DIM), hb(q_blk)),
            pl.BlockSpec((1, t, HEAD_DIM), hb(q_blk + N_GDN)),
            pl.BlockSpec((1, t, HEAD_DIM), hb(q_blk + 2 * N_GDN)),
            pl.BlockSpec((1, t, HEAD_DIM), hb(q_blk + 3 * N_GDN)),
            pl.BlockSpec((1, t, LANES), lambda i, h: (i, 0, 0)),
            pl.BlockSpec((1, 2 * SUBLANES, t), lambda i, h: (i, 0, 0)),
            pl.BlockSpec((CONV_WIDTH, HEAD_DIM), cw(0)),
            pl.BlockSpec((CONV_WIDTH, HEAD_DIM), cw(N_GDN)),
            pl.BlockSpec((CONV_WIDTH, HEAD_DIM), cw(2 * N_GDN)),
            pl.BlockSpec((1, HEAD_DIM), lambda i, h: (0, 0)),
        ],
        out_specs=pl.BlockSpec((1, t, HEAD_DIM), lambda i, h: (i, 0, h)),
        out_shape=jax.ShapeDtypeStruct((b, t, GDN_W), BF16),
        scratch_shapes=[
            pltpu.VMEM((t + GDN_PAD, 3 * HEAD_DIM), F32),
            pltpu.VMEM((HEAD_DIM, HEAD_DIM), F32),
        ],
        compiler_params=pltpu.CompilerParams(
            dimension_semantics=("arbitrary", "arbitrary"),
            vmem_limit_bytes=_vmem_limit(vmem)),
        name="gdn",
    )(p_main3, p_main3, p_main3, p_main3, gates_col, gates_row,
      conv_w, conv_w, conv_w, out_gain)


def _mem_attn_kernel(q_ref, k_ref, v_ref, gq_ref, gk_ref, o_ref, kn_ref):
    scale = HEAD_DIM ** -0.5

    @pl.when(pl.program_id(1) == 0)
    def _():
        for h in range(N_MEM):
            hs = slice(h * HEAD_DIM, (h + 1) * HEAD_DIM)
            kn_ref[:, hs] = _rms(k_ref[0, :, hs].astype(F32), gk_ref[...]).astype(BF16)

    for h in range(N_MEM):
        hs = slice(h * HEAD_DIM, (h + 1) * HEAD_DIM)
        qn = (_rms(q_ref[0, :, hs].astype(F32), gq_ref[...]) * scale).astype(BF16)
        s = lax.dot_general(qn, kn_ref[:, hs], NT_DIMS, preferred_element_type=F32)
        m = jnp.max(s, axis=-1, keepdims=True)
        p = jnp.exp(s - m)
        l = jnp.sum(p, axis=-1, keepdims=True)
        o = jnp.dot(p.astype(BF16), v_ref[0, :, hs], preferred_element_type=F32)
        o_ref[0, :, hs] = (o / l).astype(o_ref.dtype)


def _mem_attn(p_mq3, mkv3, gq, gk, tq=512):
    b, t, _ = p_mq3.shape
    m = mkv3.shape[1]
    return pl.pallas_call(
        _mem_attn_kernel,
        grid=(b, t // tq),
        in_specs=[
            pl.BlockSpec((1, tq, MEM_W), lambda i, j: (i, j, 0)),
            pl.BlockSpec((1, m, MEM_W), lambda i, j: (i, 0, 0)),
            pl.BlockSpec((1, m, MEM_W), lambda i, j: (i, 0, 1)),
            pl.BlockSpec((1, HEAD_DIM), lambda i, j: (0, 0)),
            pl.BlockSpec((1, HEAD_DIM), lambda i, j: (0, 0)),
        ],
        out_specs=pl.BlockSpec((1, tq, MEM_W), lambda i, j: (i, j, 0)),
        out_shape=jax.ShapeDtypeStruct((b, t, MEM_W), BF16),
        scratch_shapes=[pltpu.VMEM((m, MEM_W), BF16)],
        compiler_params=pltpu.CompilerParams(
            dimension_semantics=("arbitrary", "arbitrary")),
        name="mem_attn",
    )(p_mq3, mkv3, mkv3, gq, gk)


def _out_proj_kernel(x_ref, a_ref, b_ref, c_ref, wa_ref, wb_ref, wc_ref, o_ref):
    for c in range(D_MODEL // MXU_WIDTH):
        cols = slice(c * MXU_WIDTH, (c + 1) * MXU_WIDTH)
        acc = jnp.dot(a_ref[...], wa_ref[:, cols], preferred_element_type=F32)
        acc = acc + jnp.dot(b_ref[...], wb_ref[:, cols], preferred_element_type=F32)
        acc = acc + jnp.dot(c_ref[...], wc_ref[:, cols], preferred_element_type=F32)
        o_ref[:, cols] = x_ref[:, cols] + acc


def _out_proj(x2d, o_fox, o_gdn, o_mem, w_out, tm=512):
    m, d = x2d.shape
    vmem = (4 * tm * d * 4 + 2 * tm * d * 2 + d * d * 2 + 4 * tm * MXU_WIDTH * 4 + (4 << 20))
    one = pl.Buffered(1)
    return pl.pallas_call(
        _out_proj_kernel,
        grid=(m // tm,),
        in_specs=[
            pl.BlockSpec((tm, d), lambda i: (i, 0)),
            pl.BlockSpec((tm, FOX_W), lambda i: (i, 0)),
            pl.BlockSpec((tm, GDN_W), lambda i: (i, 0)),
            pl.BlockSpec((tm, MEM_W), lambda i: (i, 0)),
            pl.BlockSpec((FOX_W, d), lambda i: (0, 0), pipeline_mode=one),
            pl.BlockSpec((GDN_W, d), lambda i: (1, 0), pipeline_mode=one),
            pl.BlockSpec((MEM_W, d), lambda i: ((FOX_W + GDN_W) // MEM_W, 0), pipeline_mode=one),
        ],
        out_specs=pl.BlockSpec((tm, d), lambda i: (i, 0)),
        out_shape=jax.ShapeDtypeStruct((m, d), F32),
        compiler_params=pltpu.CompilerParams(
            dimension_semantics=("arbitrary",), vmem_limit_bytes=_vmem_limit(vmem)),
        name="out_proj",
    )(x2d, o_fox, o_gdn, o_mem, w_out, w_out, w_out)


def _ffn_kernel(h_ref, g_ref, wg_ref, wu_ref, wd_ref, o_ref, xn_ref):
    j = pl.program_id(1)

    @pl.when(j == 0)
    def _():
        _norm_rows_to(h_ref, g_ref, xn_ref, 64)
        o_ref[...] = h_ref[...]

    gate = jnp.dot(xn_ref[...], wg_ref[...], preferred_element_type=F32)
    up = jnp.dot(xn_ref[...], wu_ref[...], preferred_element_type=F32)
    act = (_silu(gate) * up).astype(BF16)
    o_ref[...] += jnp.dot(act, wd_ref[...], preferred_element_type=F32)


def _ffn(h2d, gain, w_gate_up, w_down, tm=512, tf=512):
    m, d = h2d.shape
    nf = FF_DIM // tf
    vmem = (4 * tm * d * 4 + tm * d * 2 + 4 * d * tf * 2 + 2 * tf * d * 2
            + 3 * tm * tf * 4 + tm * d * 4 + (4 << 20))
    return pl.pallas_call(
        _ffn_kernel,
        grid=(m // tm, nf),
        in_specs=[
            pl.BlockSpec((tm, d), lambda i, j: (i, 0)),
            pl.BlockSpec((1, d), lambda i, j: (0, 0)),
            pl.BlockSpec((d, tf), lambda i, j: (0, j)),
            pl.BlockSpec((d, tf), lambda i, j: (0, nf + j)),
            pl.BlockSpec((tf, d), lambda i, j: (j, 0)),
        ],
        out_specs=pl.BlockSpec((tm, d), lambda i, j: (i, 0)),
        out_shape=jax.ShapeDtypeStruct((m, d), F32),
        scratch_shapes=[pltpu.VMEM((tm, d), BF16)],
        compiler_params=pltpu.CompilerParams(
            dimension_semantics=("arbitrary", "arbitrary"),
            vmem_limit_bytes=_vmem_limit(vmem)),
        name="ffn",
    )(h2d, gain, w_gate_up, w_gate_up, w_down)


def _pack_in_proj_weights(w_in):
    o = 0
    parts = {}
    for name, width in (("fq", FOX_W), ("fk", FOX_W), ("fv", FOX_W), ("ff", N_FOX),
                        ("gqkv", 3 * GDN_W), ("gz", GDN_W), ("ga", N_GDN), ("gb", N_GDN),
                        ("mq", MEM_W)):
        parts[name] = w_in[:, o:o + width]
        o += width
    d = w_in.shape[0]
    pad = lambda w: jnp.zeros((d, w), w_in.dtype)
    small = jnp.concatenate([
        parts["ff"], pad(G_LANE - N_FOX),
        parts["ga"], pad(B_LANE - G_LANE - N_GDN),
        parts["gb"], pad(SMALL_W - B_LANE - N_GDN)], axis=1)
    w_all = jnp.concatenate([parts["fq"], parts["fk"], parts["fv"], parts["gqkv"],
                             parts["gz"], parts["mq"], small], axis=1)
    return w_all.astype(BF16)


def _lane_vec(pairs):
    v = jnp.zeros((1, LANES), F32)
    for off, vals in pairs:
        v = lax.dynamic_update_slice(v, vals.astype(F32)[None, :], (0, off))
    return v


def kernel(x, mem, norm_mix, w_in, fox_f_bias, fox_q_norm, fox_k_norm, gdn_conv, gdn_a_log,
           gdn_dt_bias, gdn_out_norm, mem_norm, w_mem_kv, mem_q_norm, mem_k_norm, w_out,
           norm_ffn, w_gate_up, w_down):
    b, t, d = x.shape
    m = mem.shape[1]
    depth = w_in.shape[0]
    h2d = x.reshape(b * t, d)
    for l in range(depth):
        row = lambda v: v[l].astype(F32)[None, :]
        w_all = _pack_in_proj_weights(w_in[l])
        p_main, p_mq, p_small = _in_proj(h2d, row(norm_mix), w_all)
        p_main3 = p_main.reshape(b, t, MAIN_W)

        addv = _lane_vec([(0, fox_f_bias[l]), (G_LANE, gdn_dt_bias[l])])
        alogv = _lane_vec([(G_LANE, gdn_a_log[l])])
        gates_col, gates_row = _gates(p_small.reshape(b, t, SMALL_W), addv, alogv)

        o_fox = _fox_attn(p_main3, gates_col, gates_row, row(fox_q_norm), row(fox_k_norm))
        o_gdn = _gdn(p_main3, gates_col, gates_row, gdn_conv[l].astype(F32), row(gdn_out_norm))

        mkv = _norm_mm(mem.reshape(b * m, d), row(mem_norm), w_mem_kv[l].astype(BF16))
        o_mem = _mem_attn(p_mq.reshape(b, t, MEM_W), mkv.reshape(b, m, 2 * MEM_W),
                          row(mem_q_norm), row(mem_k_norm))

        h1 = _out_proj(h2d, o_fox.reshape(b * t, FOX_W), o_gdn.reshape(b * t, GDN_W),
                       o_mem.reshape(b * t, MEM_W), w_out[l].astype(BF16))
        h2d = _ffn(h1, row(norm_ffn), w_gate_up[l].astype(BF16), w_down[l].astype(BF16))
    return h2d.reshape(b, t, d)
```
